```python
import math
import jax, jax.numpy as jnp
from jax import lax
import numpy as np

D_MODEL = 2048
BATCH = 1
SEQ = 16384
DEPTH = 2

CHUNK = 64
N_MIXERS = 2
N_POOL_LAYERS = (DEPTH + 1) // 2
N_ATTN_LAYERS = DEPTH // 2
POOL_WINDOWS = (2, 4, 8, 16)
POOL_GROUP = D_MODEL // len(POOL_WINDOWS)
DIFF_HEADS = 8
DIFF_HEAD_DIM = D_MODEL // (2 * DIFF_HEADS)
DIFF_V_DIM = 2 * DIFF_HEAD_DIM
Q_BLOCK = 128
REL_BUCKETS = 32
REL_MAX_DISTANCE = 128
N_EXPERTS = 32
TOP_K = 4
EXPERT_FF = D_MODEL
SWIGLU_LIMIT = 7.0
SWIGLU_ALPHA = 1.702
EXPERT_BLOCK = 256
NORM_EPS = 1e-6
SUBLN_EPS = 1e-5

kernel_name = "hybrid_pool_diffattn_moe_streaming"


def rmsnorm(x, g, eps=NORM_EPS):
    xf = x.astype(jnp.float32)
    y = xf * lax.rsqrt(jnp.mean(xf * xf, axis=-1, keepdims=True) + eps)
    return (y * g.astype(jnp.float32)).astype(x.dtype)


def pool_mixer(h, w_pool, scale):
    B, S, D = h.shape
    hf = h.astype(jnp.float32)
    cs = jnp.pad(jnp.cumsum(hf, axis=1), ((0, 0), (1, 0), (0, 0)))
    pos = jnp.arange(S, dtype=jnp.int32)
    outs = []
    for gi, w in enumerate(POOL_WINDOWS):
        sl = slice(gi * POOL_GROUP, (gi + 1) * POOL_GROUP)
        c = cs[..., sl]
        lo = jnp.maximum(pos + 1 - w, 0)
        win_sum = c[:, 1:] - jnp.take(c, lo, axis=1)
        cnt = (pos + 1 - lo).astype(jnp.float32)
        outs.append(win_sum / cnt[None, :, None] - hf[..., sl])
    p = jnp.stack(outs, axis=2).astype(h.dtype)
    y = jnp.einsum('bsgc,gcd->bsgd', p, w_pool)
    return y.reshape(B, S, D) * scale


def rel_bucket(rel):
    nb = REL_BUCKETS // 2
    ret = (rel > 0).astype(jnp.int32) * nb
    n = jnp.abs(rel)
    max_exact = nb // 2
    large = max_exact + (jnp.log(jnp.maximum(n, 1).astype(jnp.float32) / max_exact)
                         / math.log(REL_MAX_DISTANCE / max_exact) * (nb - max_exact)).astype(jnp.int32)
    large = jnp.minimum(large, nb - 1)
    return ret + jnp.where(n < max_exact, n, large)


def diff_attention(h, w_qkv, w_o, lq1, lk1, lq2, lk2, subln_g, rel_bias, lambda_init):
    B, S, D = h.shape
    H, Dh, Dv = DIFF_HEADS, DIFF_HEAD_DIM, DIFF_V_DIM
    nb = S // Q_BLOCK
    qkv = h @ w_qkv
    q = qkv[..., :D].reshape(B, nb, Q_BLOCK, H, 2, Dh).transpose(1, 0, 3, 4, 2, 5)
    k = qkv[..., D:2 * D].reshape(B, S, H, 2, Dh).transpose(0, 2, 3, 1, 4)
    v = qkv[..., 2 * D:].reshape(B, S, H, Dv).transpose(0, 2, 1, 3)
    lam = (jnp.exp(jnp.sum(lq1.astype(jnp.float32) * lk1.astype(jnp.float32)))
           - jnp.exp(jnp.sum(lq2.astype(jnp.float32) * lk2.astype(jnp.float32))) + lambda_init)
    scale = Dh ** -0.5
    k_pos = jnp.arange(S, dtype=jnp.int32)
    table = rel_bias.astype(jnp.float32)

    def q_block(args):
        qb, start = args
        q_pos = start + jnp.arange(Q_BLOCK, dtype=jnp.int32)
        bias = table[rel_bucket(k_pos[None, :] - q_pos[:, None])].transpose(2, 0, 1)
        allowed = (k_pos // CHUNK)[None, :] <= (q_pos // CHUNK)[:, None]
        s = jnp.einsum('bhmqd,bhmkd->bhmqk', qb, k).astype(jnp.float32) * scale + bias[None, :, None]
        s = jnp.where(allowed[None, None, None], s, -1e30)
        p = jax.nn.softmax(s, axis=-1)
        attn = (p[:, :, 0] - lam * p[:, :, 1]).astype(v.dtype)
        return jnp.einsum('bhqk,bhkd->bhqd', attn, v)

    starts = jnp.arange(nb, dtype=jnp.int32) * Q_BLOCK
    o = lax.map(q_block, (q, starts))
    o = o.transpose(1, 0, 3, 2, 4).reshape(B, S, H, Dv)
    of = o.astype(jnp.float32)
    of = of * lax.rsqrt(jnp.mean(of * of, axis=-1, keepdims=True) + SUBLN_EPS) * subln_g.astype(jnp.float32)
    of = of * (1.0 - lambda_init)
    return of.astype(h.dtype).reshape(B, S, D) @ w_o


def moe_ffn(h, w_router, b_router, w_gate_up, b_gate_up, w_down, b_down):
    B, S, D = h.shape
    n_tok = B * S
    t = h.reshape(n_tok, D)
    n_assign = n_tok * TOP_K
    n_blocks = -(-(n_assign + N_EXPERTS * (EXPERT_BLOCK - 1)) // EXPERT_BLOCK)
    n_rows = n_blocks * EXPERT_BLOCK

    logits = (t @ w_router + b_router).astype(jnp.float32)
    top_logits, top_idx = lax.top_k(logits, TOP_K)
    top_w = jax.nn.softmax(top_logits, axis=-1)

    flat_e = top_idx.reshape(-1).astype(jnp.int32)
    flat_tok = jnp.repeat(jnp.arange(n_tok, dtype=jnp.int32), TOP_K)
    flat_w = top_w.reshape(-1)
    order = jnp.argsort(flat_e)
    sorted_e = flat_e[order]
    counts = jnp.bincount(flat_e, length=N_EXPERTS).astype(jnp.int32)
    padded = (counts + EXPERT_BLOCK - 1) // EXPERT_BLOCK * EXPERT_BLOCK
    starts = jnp.cumsum(counts) - counts
    padded_ends = jnp.cumsum(padded)
    padded_starts = padded_ends - padded
    dest = padded_starts[sorted_e] + jnp.arange(n_assign, dtype=jnp.int32) - starts[sorted_e]
    row_tok = jnp.zeros((n_rows,), jnp.int32).at[dest].set(flat_tok[order])
    row_w = jnp.zeros((n_rows,), jnp.float32).at[dest].set(flat_w[order])
    block_e = jnp.minimum(
        jnp.searchsorted(padded_ends, jnp.arange(n_blocks, dtype=jnp.int32) * EXPERT_BLOCK, side='right'),
        N_EXPERTS - 1)

    def expert_block(args):
        toks, wts, e = args
        xb = t[toks]
        gu = xb @ w_gate_up[e] + b_gate_up[e]
        glu = jnp.minimum(gu[:, :EXPERT_FF], SWIGLU_LIMIT)
        lin = jnp.clip(gu[:, EXPERT_FF:], -SWIGLU_LIMIT, SWIGLU_LIMIT)
        a = glu * jax.nn.sigmoid(SWIGLU_ALPHA * glu) * (lin + 1.0)
        y = a @ w_down[e] + b_down[e]
        return y.astype(jnp.float32) * wts[:, None]

    ys = lax.map(expert_block, (row_tok.reshape(n_blocks, EXPERT_BLOCK),
                                row_w.reshape(n_blocks, EXPERT_BLOCK), block_e))
    out = jax.ops.segment_sum(ys.reshape(n_rows, D), row_tok, num_segments=n_tok)
    return out.astype(h.dtype).reshape(B, S, D)


def setup_inputs(seed: int = 0) -> dict:
    key = jax.random.key(seed)
    ks = jax.random.split(key, 20)
    f32 = jnp.float32
    D, E, F = D_MODEL, N_EXPERTS, EXPERT_FF
    nrm = lambda k, shape: jax.random.normal(k, shape, f32)
    return {
        "x": nrm(ks[0], (BATCH, SEQ, D)),
        "norm_mix_g": 1.0 + 0.02 * nrm(ks[1], (DEPTH, D)),
        "norm_ffn_g": 1.0 + 0.02 * nrm(ks[2], (DEPTH, D)),
        "final_norm_g": 1.0 + 0.02 * nrm(ks[3], (D,)),
        "pool_w": nrm(ks[4], (N_POOL_LAYERS, len(POOL_WINDOWS), POOL_GROUP, POOL_GROUP)) * POOL_GROUP ** -0.5,
        "pool_scale": 1.0 + 0.1 * nrm(ks[5], (N_POOL_LAYERS, D)),
        "diff_w_qkv": nrm(ks[6], (N_ATTN_LAYERS, D, 3 * D)) * D ** -0.5,
        "diff_w_o": nrm(ks[7], (N_ATTN_LAYERS, D, D)) * D ** -0.5,
        "diff_lambda_q1": 0.1 * nrm(ks[8], (N_ATTN_LAYERS, DIFF_HEAD_DIM)),
        "diff_lambda_k1": 0.1 * nrm(ks[9], (N_ATTN_LAYERS, DIFF_HEAD_DIM)),
        "diff_lambda_q2": 0.1 * nrm(ks[10], (N_ATTN_LAYERS, DIFF_HEAD_DIM)),
        "diff_lambda_k2": 0.1 * nrm(ks[11], (N_ATTN_LAYERS, DIFF_HEAD_DIM)),
        "diff_subln_g": 1.0 + 0.02 * nrm(ks[12], (N_ATTN_LAYERS, DIFF_V_DIM)),
        "rel_bias": 0.5 * nrm(ks[13], (REL_BUCKETS, DIFF_HEADS)),
        "moe_w_router": nrm(ks[14], (DEPTH, D, E)) * D ** -0.5,
        "moe_b_router": 0.01 * nrm(ks[15], (DEPTH, E)),
        "moe_w_gate_up": nrm(ks[16], (DEPTH, E, D, 2 * F)) * D ** -0.5,
        "moe_b_gate_up": 0.02 * nrm(ks[17], (DEPTH, E, 2 * F)),
        "moe_w_down": nrm(ks[18], (DEPTH, E, F, D)) * F ** -0.5,
        "moe_b_down": 0.02 * nrm(ks[19], (DEPTH, E, D)),
    }


def reference(x, norm_mix_g, norm_ffn_g, final_norm_g, pool_w, pool_scale, diff_w_qkv, diff_w_o,
              diff_lambda_q1, diff_lambda_k1, diff_lambda_q2, diff_lambda_k2, diff_subln_g, rel_bias,
              moe_w_router, moe_b_router, moe_w_gate_up, moe_b_gate_up, moe_w_down, moe_b_down):
    h = x
    for i in range(DEPTH):
        hn = rmsnorm(h, norm_mix_g[i])
        j = i // N_MIXERS
        if i % N_MIXERS == 0:
            mix = pool_mixer(hn, pool_w[j], pool_scale[j])
        else:
            lambda_init = 0.8 - 0.6 * math.exp(-0.3 * i)
            mix = diff_attention(hn, diff_w_qkv[j], diff_w_o[j], diff_lambda_q1[j], diff_lambda_k1[j],
                                 diff_lambda_q2[j], diff_lambda_k2[j], diff_subln_g[j], rel_bias, lambda_init)
        h = h + mix
        h = h + moe_ffn(rmsnorm(h, norm_ffn_g[i]), moe_w_router[i], moe_b_router[i], moe_w_gate_up[i],
                        moe_b_gate_up[i], moe_w_down[i], moe_b_down[i])
    return rmsnorm(h, final_norm_g)
```

```python
import functools
import math

import jax
import jax.numpy as jnp
from jax import lax
from jax.experimental import pallas as pl
from jax.experimental.pallas import tpu as pltpu

F32 = jnp.float32
BF16 = jnp.bfloat16
I32 = jnp.int32

TOP_K = 4
CHUNK = 64
POOL_WINDOWS = (2, 4, 8, 16)
REL_BUCKETS = 32
REL_MAX_DISTANCE = 128
SWIGLU_LIMIT = 7.0
SWIGLU_ALPHA = 1.702
NORM_EPS = 1e-6
SUBLN_EPS = 1e-5

LANES = 128
SUBLANES = 8
VMEM_LIMIT_BYTES = 56 * 1024 * 1024
ROW_TILE = 256
HALO = 16
FF_TILE = 512
ATT_TILE = 256
QKV_ROW_TILE = 1024
QKV_COL_TILE = 768
MASK_VALUE = -1e30
M_INIT = -1e37


def _cparams(sem):
    return pltpu.CompilerParams(dimension_semantics=sem, vmem_limit_bytes=VMEM_LIMIT_BYTES)


def _rms(x, g, eps):
    return x * lax.rsqrt(jnp.mean(x * x, axis=-1, keepdims=True) + eps) * g


def _route_tail(h, gf_ref, wr_ref, br_ref, carry_ref, hn_ref, eidx_ref, wts_ref, cnt_ref):
    tm = h.shape[0]
    ep = wr_ref.shape[1]
    hn = _rms(h, gf_ref[...], NORM_EPS)
    hn_ref[...] = hn
    logits = jnp.dot(hn.astype(BF16), wr_ref[...], preferred_element_type=F32) + br_ref[...]
    lane = lax.broadcasted_iota(I32, (tm, ep), 1)
    work = logits
    sel_idx, sel_val, sel_oh = [], [], []
    for _ in range(TOP_K):
        m = jnp.max(work, axis=-1, keepdims=True)
        idx = jnp.min(jnp.where(work == m, lane, ep), axis=-1, keepdims=True)
        oh = lane == idx
        work = jnp.where(oh, -jnp.inf, work)
        sel_idx.append(idx)
        sel_val.append(m)
        sel_oh.append(oh)
    ex = [jnp.exp(v - sel_val[0]) for v in sel_val]
    den = ex[0] + ex[1] + ex[2] + ex[3]
    oh_all = jnp.zeros((tm, ep), F32)
    for oh in sel_oh:
        oh_all = oh_all + oh.astype(F32)
    r = lax.broadcasted_iota(I32, (tm, tm), 0)
    c = lax.broadcasted_iota(I32, (tm, tm), 1)
    tri = jnp.where(c < r, 1.0, 0.0).astype(BF16)
    before = jnp.dot(tri, oh_all.astype(BF16), preferred_element_type=F32) + carry_ref[...]
    eout = jnp.zeros((tm, LANES), I32)
    wout = jnp.zeros((tm, LANES), F32)
    lane_o = lax.broadcasted_iota(I32, (tm, LANES), 1)
    for k in range(TOP_K):
        rank = jnp.sum(jnp.where(sel_oh[k], before, 0.0), axis=-1, keepdims=True).astype(I32)
        eout = jnp.where(lane_o == k, sel_idx[k], eout)
        eout = jnp.where(lane_o == TOP_K + k, rank, eout)
        wout = jnp.where(lane_o == k, ex[k] / den, wout)
    eidx_ref[...] = eout
    wts_ref[...] = wout
    carry_ref[...] = carry_ref[...] + jnp.sum(oh_all, axis=0, keepdims=True)
    cnt_ref[...] = carry_ref[...]


def _route_out_shapes(n, d, ep):
    return (jax.ShapeDtypeStruct((n, d), F32),
            jax.ShapeDtypeStruct((n, d), F32),
            jax.ShapeDtypeStruct((n, LANES), I32),
            jax.ShapeDtypeStruct((n, LANES), F32),
            jax.ShapeDtypeStruct((1, ep), F32))


def _route_out_specs(tm, d, ep):
    return (pl.BlockSpec((tm, d), lambda i: (i, 0)),
            pl.BlockSpec((tm, d), lambda i: (i, 0)),
            pl.BlockSpec((tm, LANES), lambda i: (i, 0)),
            pl.BlockSpec((tm, LANES), lambda i: (i, 0)),
            pl.BlockSpec((1, ep), lambda i: (0, 0)))


def _pool_mixer_kernel(x_ref, xh_ref, gm_ref, pw_ref, ps_ref, gf_ref, wr_ref, br_ref,
                       h_ref, hn_ref, eidx_ref, wts_ref, cnt_ref, ext_ref, carry_ref):
    i = pl.program_id(0)
    tm, d = x_ref.shape
    g = d // len(POOL_WINDOWS)

    @pl.when(i == 0)
    def _():
        carry_ref[...] = jnp.zeros_like(carry_ref)

    x = x_ref[...]
    gm = gm_ref[...]
    hn = _rms(x, gm, NORM_EPS)
    halo = _rms(xh_ref[...], gm, NORM_EPS)
    ext_ref[0:HALO, :] = jnp.where(i == 0, 0.0, halo)
    ext_ref[HALO:, :] = hn
    pos = i * tm + lax.broadcasted_iota(I32, (tm, 1), 0)
    outs = []
    for gi, w in enumerate(POOL_WINDOWS):
        c0 = gi * g
        tok = hn[:, c0:c0 + g]
        acc = tok
        for j in range(1, w):
            acc = acc + ext_ref[HALO - j:HALO - j + tm, c0:c0 + g]
        cnt = jnp.minimum(pos + 1, w).astype(F32)
        p = acc / cnt - tok
        outs.append(jnp.dot(p.astype(BF16), pw_ref[gi], preferred_element_type=F32))
    h = x + jnp.concatenate(outs, axis=-1) * ps_ref[...]
    h_ref[...] = h
    _route_tail(h, gf_ref, wr_ref, br_ref, carry_ref, hn_ref, eidx_ref, wts_ref, cnt_ref)


def _pool_mixer(x, g_mix, pool_w, pool_scale, g_ffn, wr, br):
    n, d = x.shape
    tm = ROW_TILE
    ep = wr.shape[1]
    ng, g, _ = pool_w.shape
    hb = tm // HALO
    return pl.pallas_call(
        _pool_mixer_kernel,
        out_shape=_route_out_shapes(n, d, ep),
        grid=(n // tm,),
        in_specs=[pl.BlockSpec((tm, d), lambda i: (i, 0)),
                  pl.BlockSpec((HALO, d), lambda i: (jnp.maximum(i * hb - 1, 0), 0)),
                  pl.BlockSpec((1, d), lambda i: (0, 0)),
                  pl.BlockSpec((ng, g, g), lambda i: (0, 0, 0)),
                  pl.BlockSpec((1, d), lambda i: (0, 0)),
                  pl.BlockSpec((1, d), lambda i: (0, 0)),
                  pl.BlockSpec((d, ep), lambda i: (0, 0)),
                  pl.BlockSpec((1, ep), lambda i: (0, 0))],
        out_specs=_route_out_specs(tm, d, ep),
        scratch_shapes=[pltpu.VMEM((HALO + tm, d), F32), pltpu.VMEM((1, ep), F32)],
        compiler_params=_cparams(("arbitrary",)),
        name="pool_mixer",
    )(x, x, g_mix, pool_w, pool_scale, g_ffn, wr, br)


def _dispatch_kernel(dest_ref, x_ref, xg_in_ref, xg_ref, sem):
    del xg_in_ref
    tm = x_ref.shape[0]

    def row_copy(t, k):
        d = dest_ref[0, 0, TOP_K * t + k]
        return pltpu.make_async_copy(x_ref.at[pl.ds(t, 1)], xg_ref.at[pl.ds(d, 1)], sem)

    def start(t, carry):
        for k in range(TOP_K):
            row_copy(t, k).start()
        return carry

    def wait(t, carry):
        for k in range(TOP_K):
            row_copy(t, k).wait()
        return carry

    lax.fori_loop(0, tm, start, 0)
    lax.fori_loop(0, tm, wait, 0)


def _dispatch(dest, hn, n_rows):
    n, d = hn.shape
    tm = ROW_TILE
    zeros = jnp.zeros((n_rows, d), hn.dtype)
    return pl.pallas_call(
        _dispatch_kernel,
        out_shape=jax.ShapeDtypeStruct((n_rows, d), hn.dtype),
        grid=(n // tm,),
        in_specs=[pl.BlockSpec((1, 1, TOP_K * tm), lambda i: (i, 0, 0), memory_space=pltpu.SMEM),
                  pl.BlockSpec((tm, d), lambda i: (i, 0)),
                  pl.BlockSpec(memory_space=pl.ANY)],
        out_specs=pl.BlockSpec(memory_space=pl.ANY),
        scratch_shapes=[pltpu.SemaphoreType.DMA],
        input_output_aliases={2: 0},
        compiler_params=_cparams(("arbitrary",)),
        name="moe_dispatch",
    )(dest.reshape(n // tm, 1, TOP_K * tm), hn, zeros)


def _expert_changed(be_ref, i):
    return jnp.logical_or(i == 0, be_ref[i] != be_ref[jnp.maximum(i - 1, 0)])


def _gate_up_kernel(be_ref, nu_ref, x_ref, wg_ref, wl_ref, bg_ref, bl_ref, a_ref, wg_bf, wl_bf):
    i = pl.program_id(1)

    @pl.when(i < nu_ref[0])
    def _():
        @pl.when(_expert_changed(be_ref, i))
        def _():
            wg_bf[...] = wg_ref[...].astype(BF16)
            wl_bf[...] = wl_ref[...].astype(BF16)

        x = x_ref[...].astype(BF16)
        gate = jnp.dot(x, wg_bf[...], preferred_element_type=F32) + bg_ref[...]
        lin = jnp.dot(x, wl_bf[...], preferred_element_type=F32) + bl_ref[...]
        glu = jnp.minimum(gate, SWIGLU_LIMIT)
        lin = jnp.clip(lin, -SWIGLU_LIMIT, SWIGLU_LIMIT)
        a_ref[...] = (glu * jax.nn.sigmoid(SWIGLU_ALPHA * glu) * (lin + 1.0)).astype(a_ref.dtype)

    @pl.when(i >= nu_ref[0])
    def _():
        a_ref[...] = jnp.zeros_like(a_ref)


def _gate_up(block_e, n_used, xg, w_gate_up, b_gate_up):
    n_rows, d = xg.shape
    e, _, f2 = w_gate_up.shape
    f = f2 // 2
    rb, tn = ROW_TILE, min(FF_TILE, f)
    nj = f // tn

    def row(i, nu):
        return jnp.minimum(i, nu[0] - 1)

    grid_spec = pltpu.PrefetchScalarGridSpec(
        num_scalar_prefetch=2,
        grid=(nj, n_rows // rb),
        in_specs=[pl.BlockSpec((rb, d), lambda j, i, be, nu: (row(i, nu), 0)),
                  pl.BlockSpec((None, d, tn), lambda j, i, be, nu: (be[row(i, nu)], 0, j)),
                  pl.BlockSpec((None, d, tn), lambda j, i, be, nu: (be[row(i, nu)], 0, j + nj)),
                  pl.BlockSpec((None, 1, tn), lambda j, i, be, nu: (be[row(i, nu)], 0, j)),
                  pl.BlockSpec((None, 1, tn), lambda j, i, be, nu: (be[row(i, nu)], 0, j + nj))],
        out_specs=pl.BlockSpec((rb, tn), lambda j, i, be, nu: (i, j)),
        scratch_shapes=[pltpu.VMEM((d, tn), BF16), pltpu.VMEM((d, tn), BF16)])
    b3 = b_gate_up.reshape(e, 1, f2)
    return pl.pallas_call(
        _gate_up_kernel,
        out_shape=jax.ShapeDtypeStruct((n_rows, f), BF16),
        grid_spec=grid_spec,
        compiler_params=_cparams(("arbitrary", "arbitrary")),
        name="moe_gate_up",
    )(block_e, n_used, xg, w_gate_up, w_gate_up, b3, b3)


def _down_kernel(be_ref, nu_ref, a_ref, wd_ref, bd_ref, y_ref, wd_bf):
    i = pl.program_id(1)

    @pl.when(i < nu_ref[0])
    def _():
        @pl.when(_expert_changed(be_ref, i))
        def _():
            wd_bf[...] = wd_ref[...].astype(BF16)

        y_ref[...] = jnp.dot(a_ref[...], wd_bf[...], preferred_element_type=F32) + bd_ref[...]

    @pl.when(i >= nu_ref[0])
    def _():
        y_ref[...] = jnp.zeros_like(y_ref)


def _down(block_e, n_used, a, w_down, b_down):
    n_rows, f = a.shape
    e, _, d = w_down.shape
    rb, tn = ROW_TILE, min(FF_TILE, d)

    def row(i, nu):
        return jnp.minimum(i, nu[0] - 1)

    grid_spec = pltpu.PrefetchScalarGridSpec(
        num_scalar_prefetch=2,
        grid=(d // tn, n_rows // rb),
        in_specs=[pl.BlockSpec((rb, f), lambda j, i, be, nu: (row(i, nu), 0)),
                  pl.BlockSpec((None, f, tn), lambda j, i, be, nu: (be[row(i, nu)], 0, j)),
                  pl.BlockSpec((None, 1, tn), lambda j, i, be, nu: (be[row(i, nu)], 0, j))],
        out_specs=pl.BlockSpec((rb, tn), lambda j, i, be, nu: (i, j)),
        scratch_shapes=[pltpu.VMEM((f, tn), BF16)])
    return pl.pallas_call(
        _down_kernel,
        out_shape=jax.ShapeDtypeStruct((n_rows, d), F32),
        grid_spec=grid_spec,
        compiler_params=_cparams(("arbitrary", "arbitrary")),
        name="moe_down",
    )(block_e, n_used, a, w_down, b_down.reshape(e, 1, d))


def _combine_kernel(dcur_ref, dnext_ref, wts_ref, h_ref, g_ref, y_ref, *rest, emit_h):
    if emit_h:
        h_out_ref, hn_out_ref, buf, sem = rest
    else:
        hn_out_ref, buf, sem = rest
    i = pl.program_id(0)
    n = pl.num_programs(0)
    tm = h_ref.shape[0]

    def row_copy(dest_ref, slot, t, k):
        d = dest_ref[0, 0, TOP_K * t + k]
        return pltpu.make_async_copy(y_ref.at[pl.ds(d, 1)], buf.at[slot, k, pl.ds(t, 1)], sem.at[slot])

    def start_tile(dest_ref, slot):
        def body(t, carry):
            for k in range(TOP_K):
                row_copy(dest_ref, slot, t, k).start()
            return carry
        lax.fori_loop(0, tm, body, 0)

    @pl.when(i == 0)
    def _():
        start_tile(dcur_ref, 0)

    @pl.when(i + 1 < n)
    def _():
        start_tile(dnext_ref, (i + 1) % 2)

    slot = i % 2

    def wait_body(t, carry):
        for k in range(TOP_K):
            row_copy(dcur_ref, slot, t, k).wait()
        return carry
    lax.fori_loop(0, tm, wait_body, 0)

    w = wts_ref[...]
    acc = h_ref[...]
    for k in range(TOP_K):
        acc = acc + w[:, k:k + 1] * buf[slot, k]
    if emit_h:
        h_out_ref[...] = acc
    hn_out_ref[...] = _rms(acc, g_ref[...], NORM_EPS).astype(hn_out_ref.dtype)


def _combine(dest, wts, h, g_next, y, *, emit_h, hn_dtype):
    n, d = h.shape
    tm = ROW_TILE
    nt = n // tm
    dest3 = dest.reshape(nt, 1, TOP_K * tm)
    tile = pl.BlockSpec((tm, d), lambda i: (i, 0))
    out_shape = [jax.ShapeDtypeStruct((n, d), hn_dtype)]
    out_specs = [tile]
    if emit_h:
        out_shape = [jax.ShapeDtypeStruct((n, d), F32)] + out_shape
        out_specs = [tile] + out_specs
    return pl.pallas_call(
        functools.partial(_combine_kernel, emit_h=emit_h),
        out_shape=tuple(out_shape),
        grid=(nt,),
        in_specs=[pl.BlockSpec((1, 1, TOP_K * tm), lambda i: (i, 0, 0), memory_space=pltpu.SMEM),
                  pl.BlockSpec((1, 1, TOP_K * tm), lambda i: (jnp.minimum(i + 1, nt - 1), 0, 0),
                               memory_space=pltpu.SMEM),
                  pl.BlockSpec((tm, LANES), lambda i: (i, 0)),
                  tile,
                  pl.BlockSpec((1, d), lambda i: (0, 0)),
                  pl.BlockSpec(memory_space=pl.ANY)],
        out_specs=tuple(out_specs),
        scratch_shapes=[pltpu.VMEM((2, TOP_K, tm, d), F32), pltpu.SemaphoreType.DMA((2,))],
        compiler_params=_cparams(("arbitrary",)),
        name="moe_combine",
    )(dest3, dest3, wts, h, g_next, y)


def _moe(h, hn, eidx, wts, cnt, w_gate_up, b_gate_up, w_down, b_down, g_next, *, emit_h, hn_dtype):
    n, _ = h.shape
    e = w_gate_up.shape[0]
    rb = ROW_TILE
    n_blocks = -(-(n * TOP_K + e * (rb - 1)) // rb)
    counts = cnt[0, :e].astype(I32)
    padded = (counts + rb - 1) // rb * rb
    pend = jnp.cumsum(padded)
    pstart = pend - padded
    dest = pstart[eidx[:, :TOP_K]] + eidx[:, TOP_K:2 * TOP_K]
    block_e = jnp.minimum(jnp.searchsorted(pend, jnp.arange(n_blocks, dtype=I32) * rb, side='right'),
                          e - 1).astype(I32)
    n_used = (pend[-1:] // rb).astype(I32)
    xg = _dispatch(dest, hn, n_blocks * rb)
    a = _gate_up(block_e, n_used, xg, w_gate_up, b_gate_up)
    y = _down(block_e, n_used, a, w_down, b_down)
    return _combine(dest, wts, h, g_next, y, emit_h=emit_h, hn_dtype=hn_dtype)


def _matmul_kernel(x_ref, w_ref, o_ref):
    o_ref[...] = jnp.dot(x_ref[...], w_ref[...], preferred_element_type=F32).astype(o_ref.dtype)


def _qkv_proj(x, w):
    n, d = x.shape
    n3 = w.shape[1]
    tm, tn = min(QKV_ROW_TILE, n), min(QKV_COL_TILE, n3)
    return pl.pallas_call(
        _matmul_kernel,
        out_shape=jax.ShapeDtypeStruct((n, n3), BF16),
        grid=(n3 // tn, n // tm),
        in_specs=[pl.BlockSpec((tm, d), lambda j, i: (i, 0)),
                  pl.BlockSpec((d, tn), lambda j, i: (0, j))],
        out_specs=pl.BlockSpec((tm, tn), lambda j, i: (i, j)),
        compiler_params=_cparams(("parallel", "parallel")),
        name="qkv_proj",
    )(x, w)


def _attn_out_kernel(a_ref, wo_ref, h_ref, gf_ref, wr_ref, br_ref,
                     ho_ref, hn_ref, eidx_ref, wts_ref, cnt_ref, carry_ref):
    @pl.when(pl.program_id(0) == 0)
    def _():
        carry_ref[...] = jnp.zeros_like(carry_ref)

    h = h_ref[...] + jnp.dot(a_ref[...], wo_ref[...], preferred_element_type=F32)
    ho_ref[...] = h
    _route_tail(h, gf_ref, wr_ref, br_ref, carry_ref, hn_ref, eidx_ref, wts_ref, cnt_ref)


def _attn_out(a, wo, h, g_ffn, wr, br):
    n, d = h.shape
    tm = ROW_TILE
    ep = wr.shape[1]
    return pl.pallas_call(
        _attn_out_kernel,
        out_shape=_route_out_shapes(n, d, ep),
        grid=(n // tm,),
        in_specs=[pl.BlockSpec((tm, d), lambda i: (i, 0)),
                  pl.BlockSpec((d, d), lambda i: (0, 0)),
                  pl.BlockSpec((tm, d), lambda i: (i, 0)),
                  pl.BlockSpec((1, d), lambda i: (0, 0)),
                  pl.BlockSpec((d, ep), lambda i: (0, 0)),
                  pl.BlockSpec((1, ep), lambda i: (0, 0))],
        out_specs=_route_out_specs(tm, d, ep),
        scratch_shapes=[pltpu.VMEM((1, ep), F32)],
        compiler_params=_cparams(("arbitrary",)),
        name="attn_out_proj",
    )(a, wo, h, g_ffn, wr, br)


def _attention_kernel(q_ref, k_ref, v_ref, bias_ref, g_ref, lam_ref, o_ref,
                      m_ref, l_ref, acc_ref, *, exp_mult, out_mult):
    i = pl.program_id(1)
    tq = q_ref.shape[0]
    tk = tq
    dh = q_ref.shape[1] // 2
    nt = (((1,), (1,)), ((), ()))
    q = (q_ref[:, :dh], q_ref[:, dh:])

    m_ref[...] = jnp.full_like(m_ref, M_INIT)
    l_ref[...] = jnp.zeros_like(l_ref)
    acc_ref[...] = jnp.zeros_like(acc_ref)

    def tile(j, bias):
        start = pl.multiple_of(j * tk, tk)
        kt = k_ref[pl.ds(start, tk), :]
        vt = v_ref[pl.ds(start, tk), :]
        for mp in range(2):
            s = lax.dot_general(q[mp], kt[:, mp * dh:(mp + 1) * dh], nt, preferred_element_type=F32)
            if bias is not None:
                s = s + bias
            m_old = m_ref[mp]
            m_new = jnp.maximum(m_old, jnp.max(s, axis=-1, keepdims=True))
            p = jnp.exp2((s - m_new) * exp_mult)
            alpha = jnp.exp2((m_old - m_new) * exp_mult)
            l_ref[mp] = alpha * l_ref[mp] + jnp.sum(p, axis=-1, keepdims=True)
            acc_ref[mp] = alpha * acc_ref[mp] + jnp.dot(p.astype(BF16), vt, preferred_element_type=F32)
            m_ref[mp] = m_new

    def far(j, carry):
        tile(j, None)
        return carry

    lax.fori_loop(0, i - 1, far, 0)

    @pl.when(i >= 1)
    def _():
        tile(i - 1, bias_ref[:, :tk])

    tile(i, bias_ref[:, tk:])

    lam = lam_ref[0]
    o = acc_ref[0] / l_ref[0] - lam * (acc_ref[1] / l_ref[1])
    o = o * lax.rsqrt(jnp.mean(o * o, axis=-1, keepdims=True) + SUBLN_EPS) * g_ref[...]
    o_ref[...] = (o * out_mult).astype(o_ref.dtype)


def _attention(qkv, lam, bias, subln_g, *, heads, dh, lambda_init):
    n = qkv.shape[0]
    dv = 2 * dh
    d = heads * dv
    tq = ATT_TILE
    scale = dh ** -0.5
    kern = functools.partial(_attention_kernel, exp_mult=scale * math.log2(math.e),
                             out_mult=1.0 - lambda_init)
    return pl.pallas_call(
        kern,
        out_shape=jax.ShapeDtypeStruct((n, d), BF16),
        grid=(heads, n // tq),
        in_specs=[pl.BlockSpec((tq, dv), lambda h, i: (i, h)),
                  pl.BlockSpec((n, dv), lambda h, i: (0, heads + h)),
                  pl.BlockSpec((n, dv), lambda h, i: (0, 2 * heads + h)),
                  pl.BlockSpec((None, tq, 2 * tq), lambda h, i: (h, 0, 0)),
                  pl.BlockSpec((1, dv), lambda h, i: (0, 0)),
                  pl.BlockSpec(memory_space=pltpu.SMEM)],
        out_specs=pl.BlockSpec((tq, dv), lambda h, i: (i, h)),
        scratch_shapes=[pltpu.VMEM((2, tq, 1), F32), pltpu.VMEM((2, tq, 1), F32),
                        pltpu.VMEM((2, tq, dv), F32)],
        compiler_params=_cparams(("arbitrary", "arbitrary")),
        name="diff_attention",
    )(qkv, qkv, qkv, bias, subln_g, lam)


def _rel_bucket(rel):
    nb = REL_BUCKETS // 2
    ret = (rel > 0).astype(I32) * nb
    n = jnp.abs(rel)
    max_exact = nb // 2
    large = max_exact + (jnp.log(jnp.maximum(n, 1).astype(F32) / max_exact)
                         / math.log(REL_MAX_DISTANCE / max_exact) * (nb - max_exact)).astype(I32)
    large = jnp.minimum(large, nb - 1)
    return ret + jnp.where(n < max_exact, n, large)


def _near_bias(rel_bias, dh):
    tq = ATT_TILE
    scale = dh ** -0.5
    table = rel_bias.astype(F32)
    r = jnp.arange(tq, dtype=I32)[:, None]
    c = jnp.arange(2 * tq, dtype=I32)[None, :] - tq
    bias = table[_rel_bucket(c - r)]
    far = table[_rel_bucket(jnp.full((), -(2 * tq), I32))]
    allowed = jnp.floor_divide(c, CHUNK) <= (r // CHUNK)
    pat = jnp.where(allowed[:, :, None], (bias - far) / scale, MASK_VALUE / scale)
    return pat.transpose(2, 0, 1)


def kernel(x, norm_mix_g, norm_ffn_g, final_norm_g, pool_w, pool_scale, diff_w_qkv, diff_w_o,
           diff_lambda_q1, diff_lambda_k1, diff_lambda_q2, diff_lambda_k2, diff_subln_g, rel_bias,
           moe_w_router, moe_b_router, moe_w_gate_up, moe_b_gate_up, moe_w_down, moe_b_down):
    b, s, d = x.shape
    assert b == 1, "token tiles assume a single sequence"
    assert norm_mix_g.shape[0] == 2, "layer 0 pools, layer 1 attends"
    e = moe_w_router.shape[2]
    heads = rel_bias.shape[1]
    dh = diff_lambda_q1.shape[1]
    assert s % ROW_TILE == 0 and s % ATT_TILE == 0 and ATT_TILE % CHUNK == 0 and ATT_TILE >= REL_MAX_DISTANCE
    ep = -(-e // LANES) * LANES
    xt = x.reshape(s, d)

    def router(i):
        wr = jnp.pad(moe_w_router[i], ((0, 0), (0, ep - e))).astype(BF16)
        br = jnp.pad(moe_b_router[i].astype(F32), (0, ep - e), constant_values=MASK_VALUE).reshape(1, ep)
        return wr, br

    row = lambda v: v.astype(F32).reshape(1, -1)

    wr, br = router(0)
    h, hn, eidx, wts, cnt = _pool_mixer(xt, row(norm_mix_g[0]), pool_w[0].astype(BF16), row(pool_scale[0]),
                                        row(norm_ffn_g[0]), wr, br)
    h, hn = _moe(h, hn, eidx, wts, cnt, moe_w_gate_up[0], moe_b_gate_up[0], moe_w_down[0], moe_b_down[0],
                 row(norm_mix_g[1]), emit_h=True, hn_dtype=BF16)

    lambda_init = 0.8 - 0.6 * math.exp(-0.3 * 1)
    lam = (jnp.exp(jnp.sum(diff_lambda_q1[0].astype(F32) * diff_lambda_k1[0].astype(F32)))
           - jnp.exp(jnp.sum(diff_lambda_q2[0].astype(F32) * diff_lambda_k2[0].astype(F32))) + lambda_init)
    qkv = _qkv_proj(hn, diff_w_qkv[0].astype(BF16))
    att = _attention(qkv, lam.reshape(1), _near_bias(rel_bias, dh), row(diff_subln_g[0]),
                     heads=heads, dh=dh, lambda_init=lambda_init)
    wr, br = router(1)
    h, hn, eidx, wts, cnt = _attn_out(att, diff_w_o[0].astype(BF16), h, row(norm_ffn_g[1]), wr, br)
    (out,) = _moe(h, hn, eidx, wts, cnt, moe_w_gate_up[1], moe_b_gate_up[1], moe_w_down[1], moe_b_down[1],
                  row(final_norm_g), emit_h=False, hn_dtype=x.dtype)
    return out.reshape(b, s, d)
```

```python
import functools
import math

import jax
import jax.numpy as jnp
from jax import lax
from jax.experimental import pallas as pl
from jax.experimental.pallas import tpu as pltpu

F32 = jnp.float32
BF16 = jnp.bfloat16
I32 = jnp.int32

TOP_K = 4
CHUNK = 64
POOL_WINDOWS = (2, 4, 8, 16)
REL_BUCKETS = 32
REL_MAX_DISTANCE = 128
SWIGLU_LIMIT = 7.0
SWIGLU_ALPHA = 1.702
NORM_EPS = 1e-6
SUBLN_EPS = 1e-5

LANES = 128
SUBLANES = 8
VMEM_LIMIT_BYTES = 56 * 1024 * 1024
ROW_TILE = 256
HALO = 16
FF_TILE = 512
ATT_TILE = 512
QKV_ROW_TILE = 1024
QKV_COL_TILE = 768
MASK_VALUE = -1e30
M_INIT = -1e37


def _cparams(sem):
    return pltpu.CompilerParams(dimension_semantics=sem, vmem_limit_bytes=VMEM_LIMIT_BYTES)


def _rms(x, g, eps):
    return x * lax.rsqrt(jnp.mean(x * x, axis=-1, keepdims=True) + eps) * g


def _route_tail(h, gf_ref, wr_ref, br_ref, carry_ref, hn_ref, eidx_ref, wts_ref, cnt_ref):
    tm = h.shape[0]
    ep = wr_ref.shape[1]
    hn = _rms(h, gf_ref[...], NORM_EPS)
    hn_ref[...] = hn
    logits = jnp.dot(hn.astype(BF16), wr_ref[...], preferred_element_type=F32) + br_ref[...]
    lane = lax.broadcasted_iota(I32, (tm, ep), 1)
    work = logits
    sel_idx, sel_val, sel_oh = [], [], []
    for _ in range(TOP_K):
        m = jnp.max(work, axis=-1, keepdims=True)
        idx = jnp.min(jnp.where(work == m, lane, ep), axis=-1, keepdims=True)
        oh = lane == idx
        work = jnp.where(oh, -jnp.inf, work)
        sel_idx.append(idx)
        sel_val.append(m)
        sel_oh.append(oh)
    ex = [jnp.exp(v - sel_val[0]) for v in sel_val]
    den = ex[0] + ex[1] + ex[2] + ex[3]
    oh_all = jnp.zeros((tm, ep), F32)
    for oh in sel_oh:
        oh_all = oh_all + oh.astype(F32)
    r = lax.broadcasted_iota(I32, (tm, tm), 0)
    c = lax.broadcasted_iota(I32, (tm, tm), 1)
    tri = jnp.where(c < r, 1.0, 0.0).astype(BF16)
    before = jnp.dot(tri, oh_all.astype(BF16), preferred_element_type=F32) + carry_ref[...]
    eout = jnp.zeros((tm, LANES), I32)
    wout = jnp.zeros((tm, LANES), F32)
    lane_o = lax.broadcasted_iota(I32, (tm, LANES), 1)
    for k in range(TOP_K):
        rank = jnp.sum(jnp.where(sel_oh[k], before, 0.0), axis=-1, keepdims=True).astype(I32)
        eout = jnp.where(lane_o == k, sel_idx[k], eout)
        eout = jnp.where(lane_o == TOP_K + k, rank, eout)
        wout = jnp.where(lane_o == k, ex[k] / den, wout)
    eidx_ref[...] = eout
    wts_ref[...] = wout
    carry_ref[...] = carry_ref[...] + jnp.sum(oh_all, axis=0, keepdims=True)
    cnt_ref[...] = carry_ref[...]


def _route_out_shapes(n, d, ep):
    return (jax.ShapeDtypeStruct((n, d), F32),
            jax.ShapeDtypeStruct((n, d), F32),
            jax.ShapeDtypeStruct((n, LANES), I32),
            jax.ShapeDtypeStruct((n, LANES), F32),
            jax.ShapeDtypeStruct((1, ep), F32))


def _route_out_specs(tm, d, ep):
    return (pl.BlockSpec((tm, d), lambda i: (i, 0)),
            pl.BlockSpec((tm, d), lambda i: (i, 0)),
            pl.BlockSpec((tm, LANES), lambda i: (i, 0)),
            pl.BlockSpec((tm, LANES), lambda i: (i, 0)),
            pl.BlockSpec((1, ep), lambda i: (0, 0)))


def _pool_mixer_kernel(x_ref, xh_ref, gm_ref, pw_ref, ps_ref, gf_ref, wr_ref, br_ref,
                       h_ref, hn_ref, eidx_ref, wts_ref, cnt_ref, ext_ref, carry_ref):
    i = pl.program_id(0)
    tm, d = x_ref.shape
    g = d // len(POOL_WINDOWS)

    @pl.when(i == 0)
    def _():
        carry_ref[...] = jnp.zeros_like(carry_ref)

    x = x_ref[...]
    gm = gm_ref[...]
    hn = _rms(x, gm, NORM_EPS)
    halo = _rms(xh_ref[...], gm, NORM_EPS)
    ext_ref[0:HALO, :] = jnp.where(i == 0, 0.0, halo)
    ext_ref[HALO:, :] = hn
    pos = i * tm + lax.broadcasted_iota(I32, (tm, 1), 0)
    outs = []
    for gi, w in enumerate(POOL_WINDOWS):
        c0 = gi * g
        tok = hn[:, c0:c0 + g]
        acc = tok
        for j in range(1, w):
            acc = acc + ext_ref[HALO - j:HALO - j + tm, c0:c0 + g]
        cnt = jnp.minimum(pos + 1, w).astype(F32)
        p = acc / cnt - tok
        outs.append(jnp.dot(p.astype(BF16), pw_ref[gi], preferred_element_type=F32))
    h = x + jnp.concatenate(outs, axis=-1) * ps_ref[...]
    h_ref[...] = h
    _route_tail(h, gf_ref, wr_ref, br_ref, carry_ref, hn_ref, eidx_ref, wts_ref, cnt_ref)


def _pool_mixer(x, g_mix, pool_w, pool_scale, g_ffn, wr, br):
    n, d = x.shape
    tm = ROW_TILE
    ep = wr.shape[1]
    ng, g, _ = pool_w.shape
    hb = tm // HALO
    return pl.pallas_call(
        _pool_mixer_kernel,
        out_shape=_route_out_shapes(n, d, ep),
        grid=(n // tm,),
        in_specs=[pl.BlockSpec((tm, d), lambda i: (i, 0)),
                  pl.BlockSpec((HALO, d), lambda i: (jnp.maximum(i * hb - 1, 0), 0)),
                  pl.BlockSpec((1, d), lambda i: (0, 0)),
                  pl.BlockSpec((ng, g, g), lambda i: (0, 0, 0)),
                  pl.BlockSpec((1, d), lambda i: (0, 0)),
                  pl.BlockSpec((1, d), lambda i: (0, 0)),
                  pl.BlockSpec((d, ep), lambda i: (0, 0)),
                  pl.BlockSpec((1, ep), lambda i: (0, 0))],
        out_specs=_route_out_specs(tm, d, ep),
        scratch_shapes=[pltpu.VMEM((HALO + tm, d), F32), pltpu.VMEM((1, ep), F32)],
        compiler_params=_cparams(("arbitrary",)),
        name="pool_mixer",
    )(x, x, g_mix, pool_w, pool_scale, g_ffn, wr, br)


def _dispatch_kernel(fill_ref, nu_ref, dest_ref, x_ref, xg_ref, zero_ref, sem, zsem):
    tm = x_ref.shape[0]
    n_blocks = xg_ref.shape[0] // tm

    @pl.when(pl.program_id(0) == 0)
    def _():
        zero_ref[...] = jnp.zeros_like(zero_ref)

        def zero_rows(first):
            cp = pltpu.make_async_copy(zero_ref, xg_ref.at[pl.ds(first, tm)], zsem)
            cp.start()
            cp.wait()

        def pad(e, carry):
            zero_rows(pl.multiple_of(fill_ref[e], tm))
            return carry

        def tail(b, carry):
            zero_rows(pl.multiple_of(b * tm, tm))
            return carry

        lax.fori_loop(0, fill_ref.shape[0], pad, 0)
        lax.fori_loop(nu_ref[0], n_blocks, tail, 0)

    def row_copy(t, k):
        d = dest_ref[0, 0, TOP_K * t + k]
        return pltpu.make_async_copy(x_ref.at[pl.ds(t, 1)], xg_ref.at[pl.ds(d, 1)], sem)

    def start(t, carry):
        for k in range(TOP_K):
            row_copy(t, k).start()
        return carry

    def wait(t, carry):
        for k in range(TOP_K):
            row_copy(t, k).wait()
        return carry

    lax.fori_loop(0, tm, start, 0)
    lax.fori_loop(0, tm, wait, 0)


def _dispatch(fill_start, n_used, dest, hn, n_rows):
    n, d = hn.shape
    tm = ROW_TILE
    grid_spec = pltpu.PrefetchScalarGridSpec(
        num_scalar_prefetch=2,
        grid=(n // tm,),
        in_specs=[pl.BlockSpec((1, 1, TOP_K * tm), lambda i, fs, nu: (i, 0, 0), memory_space=pltpu.SMEM),
                  pl.BlockSpec((tm, d), lambda i, fs, nu: (i, 0))],
        out_specs=pl.BlockSpec(memory_space=pl.ANY),
        scratch_shapes=[pltpu.VMEM((tm, d), hn.dtype), pltpu.SemaphoreType.DMA, pltpu.SemaphoreType.DMA])
    return pl.pallas_call(
        _dispatch_kernel,
        out_shape=jax.ShapeDtypeStruct((n_rows, d), hn.dtype),
        grid_spec=grid_spec,
        compiler_params=_cparams(("arbitrary",)),
        name="moe_dispatch",
    )(fill_start, n_used, dest.reshape(n // tm, 1, TOP_K * tm), hn)


def _expert_changed(be_ref, i):
    return jnp.logical_or(i == 0, be_ref[i] != be_ref[jnp.maximum(i - 1, 0)])


def _gate_up_kernel(be_ref, nu_ref, x_ref, wg_ref, wl_ref, bg_ref, bl_ref, a_ref, wg_bf, wl_bf):
    i = pl.program_id(1)

    @pl.when(i < nu_ref[0])
    def _():
        @pl.when(_expert_changed(be_ref, i))
        def _():
            wg_bf[...] = wg_ref[...].astype(BF16)
            wl_bf[...] = wl_ref[...].astype(BF16)

        x = x_ref[...].astype(BF16)
        gate = jnp.dot(x, wg_bf[...], preferred_element_type=F32) + bg_ref[...]
        lin = jnp.dot(x, wl_bf[...], preferred_element_type=F32) + bl_ref[...]
        glu = jnp.minimum(gate, SWIGLU_LIMIT)
        lin = jnp.clip(lin, -SWIGLU_LIMIT, SWIGLU_LIMIT)
        a_ref[...] = (glu * jax.nn.sigmoid(SWIGLU_ALPHA * glu) * (lin + 1.0)).astype(a_ref.dtype)

    @pl.when(i >= nu_ref[0])
    def _():
        a_ref[...] = jnp.zeros_like(a_ref)


def _gate_up(block_e, n_used, xg, w_gate_up, b_gate_up, layer):
    n_rows, d = xg.shape
    nl, e, _, f2 = w_gate_up.shape
    f = f2 // 2
    rb, tn = ROW_TILE, min(FF_TILE, f)
    nj = f // tn

    def row(i, nu):
        return jnp.minimum(i, nu[0] - 1)

    grid_spec = pltpu.PrefetchScalarGridSpec(
        num_scalar_prefetch=2,
        grid=(nj, n_rows // rb),
        in_specs=[pl.BlockSpec((rb, d), lambda j, i, be, nu: (row(i, nu), 0)),
                  pl.BlockSpec((None, None, d, tn), lambda j, i, be, nu: (layer, be[row(i, nu)], 0, j)),
                  pl.BlockSpec((None, None, d, tn), lambda j, i, be, nu: (layer, be[row(i, nu)], 0, j + nj)),
                  pl.BlockSpec((None, None, 1, tn), lambda j, i, be, nu: (layer, be[row(i, nu)], 0, j)),
                  pl.BlockSpec((None, None, 1, tn), lambda j, i, be, nu: (layer, be[row(i, nu)], 0, j + nj))],
        out_specs=pl.BlockSpec((rb, tn), lambda j, i, be, nu: (i, j)),
        scratch_shapes=[pltpu.VMEM((d, tn), BF16), pltpu.VMEM((d, tn), BF16)])
    b3 = b_gate_up.reshape(nl, e, 1, f2)
    return pl.pallas_call(
        _gate_up_kernel,
        out_shape=jax.ShapeDtypeStruct((n_rows, f), BF16),
        grid_spec=grid_spec,
        compiler_params=_cparams(("arbitrary", "arbitrary")),
        name="moe_gate_up",
    )(block_e, n_used, xg, w_gate_up, w_gate_up, b3, b3)


def _down_kernel(be_ref, nu_ref, a_ref, wd_ref, bd_ref, y_ref, wd_bf):
    i = pl.program_id(1)

    @pl.when(i < nu_ref[0])
    def _():
        @pl.when(_expert_changed(be_ref, i))
        def _():
            wd_bf[...] = wd_ref[...].astype(BF16)

        y_ref[...] = jnp.dot(a_ref[...], wd_bf[...], preferred_element_type=F32) + bd_ref[...]

    @pl.when(i >= nu_ref[0])
    def _():
        y_ref[...] = jnp.zeros_like(y_ref)


def _down(block_e, n_used, a, w_down, b_down, layer):
    n_rows, f = a.shape
    nl, e, _, d = w_down.shape
    rb, tn = ROW_TILE, min(FF_TILE, d)

    def row(i, nu):
        return jnp.minimum(i, nu[0] - 1)

    grid_spec = pltpu.PrefetchScalarGridSpec(
        num_scalar_prefetch=2,
        grid=(d // tn, n_rows // rb),
        in_specs=[pl.BlockSpec((rb, f), lambda j, i, be, nu: (row(i, nu), 0)),
                  pl.BlockSpec((None, None, f, tn), lambda j, i, be, nu: (layer, be[row(i, nu)], 0, j)),
                  pl.BlockSpec((None, None, 1, tn), lambda j, i, be, nu: (layer, be[row(i, nu)], 0, j))],
        out_specs=pl.BlockSpec((rb, tn), lambda j, i, be, nu: (i, j)),
        scratch_shapes=[pltpu.VMEM((f, tn), BF16)])
    return pl.pallas_call(
        _down_kernel,
        out_shape=jax.ShapeDtypeStruct((n_rows, d), F32),
        grid_spec=grid_spec,
        compiler_params=_cparams(("arbitrary", "arbitrary")),
        name="moe_down",
    )(block_e, n_used, a, w_down, b_down.reshape(nl, e, 1, d))


def _combine_kernel(dcur_ref, dnext_ref, wts_ref, h_ref, g_ref, y_ref, *rest, emit_h):
    if emit_h:
        h_out_ref, hn_out_ref, buf, sem = rest
    else:
        hn_out_ref, buf, sem = rest
    i = pl.program_id(0)
    n = pl.num_programs(0)
    tm = h_ref.shape[0]

    def row_copy(dest_ref, slot, t, k):
        d = dest_ref[0, 0, TOP_K * t + k]
        return pltpu.make_async_copy(y_ref.at[pl.ds(d, 1)], buf.at[slot, k, pl.ds(t, 1)], sem.at[slot])

    def start_tile(dest_ref, slot):
        def body(t, carry):
            for k in range(TOP_K):
                row_copy(dest_ref, slot, t, k).start()
            return carry
        lax.fori_loop(0, tm, body, 0)

    @pl.when(i == 0)
    def _():
        start_tile(dcur_ref, 0)

    @pl.when(i + 1 < n)
    def _():
        start_tile(dnext_ref, (i + 1) % 2)

    slot = i % 2

    def wait_body(t, carry):
        for k in range(TOP_K):
            row_copy(dcur_ref, slot, t, k).wait()
        return carry
    lax.fori_loop(0, tm, wait_body, 0)

    w = wts_ref[...]
    acc = h_ref[...]
    for k in range(TOP_K):
        acc = acc + w[:, k:k + 1] * buf[slot, k]
    if emit_h:
        h_out_ref[...] = acc
    hn_out_ref[...] = _rms(acc, g_ref[...], NORM_EPS).astype(hn_out_ref.dtype)


def _combine(dest, wts, h, g_next, y, *, emit_h, hn_dtype):
    n, d = h.shape
    tm = ROW_TILE
    nt = n // tm
    dest3 = dest.reshape(nt, 1, TOP_K * tm)
    tile = pl.BlockSpec((tm, d), lambda i: (i, 0))
    out_shape = [jax.ShapeDtypeStruct((n, d), hn_dtype)]
    out_specs = [tile]
    if emit_h:
        out_shape = [jax.ShapeDtypeStruct((n, d), F32)] + out_shape
        out_specs = [tile] + out_specs
    return pl.pallas_call(
        functools.partial(_combine_kernel, emit_h=emit_h),
        out_shape=tuple(out_shape),
        grid=(nt,),
        in_specs=[pl.BlockSpec((1, 1, TOP_K * tm), lambda i: (i, 0, 0), memory_space=pltpu.SMEM),
                  pl.BlockSpec((1, 1, TOP_K * tm), lambda i: (jnp.minimum(i + 1, nt - 1), 0, 0),
                               memory_space=pltpu.SMEM),
                  pl.BlockSpec((tm, LANES), lambda i: (i, 0)),
                  tile,
                  pl.BlockSpec((1, d), lambda i: (0, 0)),
                  pl.BlockSpec(memory_space=pl.ANY)],
        out_specs=tuple(out_specs),
        scratch_shapes=[pltpu.VMEM((2, TOP_K, tm, d), F32), pltpu.SemaphoreType.DMA((2,))],
        compiler_params=_cparams(("arbitrary",)),
        name="moe_combine",
    )(dest3, dest3, wts, h, g_next, y)


def _moe(h, hn, eidx, wts, cnt, w_gate_up, b_gate_up, w_down, b_down, layer, g_next, *, emit_h, hn_dtype):
    n, _ = h.shape
    e = w_gate_up.shape[1]
    rb = ROW_TILE
    n_blocks = -(-(n * TOP_K + e * (rb - 1)) // rb)
    counts = cnt[0, :e].astype(I32)
    padded = (counts + rb - 1) // rb * rb
    ids = jnp.arange(e, dtype=I32)
    pend = jnp.sum(jnp.where(ids[None, :] <= ids[:, None], padded[None, :], 0), axis=1)
    pstart = pend - padded
    sel = eidx[:, :TOP_K, None] == ids[None, None, :]
    dest = jnp.sum(jnp.where(sel, pstart[None, None, :], 0), axis=-1) + eidx[:, TOP_K:2 * TOP_K]
    first_row = jnp.arange(n_blocks, dtype=I32) * rb
    block_e = jnp.minimum(jnp.sum((pend[None, :] <= first_row[:, None]).astype(I32), axis=1), e - 1)
    n_used = pend[-1:] // rb
    xg = _dispatch(jnp.maximum(pend - rb, 0), n_used, dest, hn, n_blocks * rb)
    a = _gate_up(block_e, n_used, xg, w_gate_up, b_gate_up, layer)
    y = _down(block_e, n_used, a, w_down, b_down, layer)
    return _combine(dest, wts, h, g_next, y, emit_h=emit_h, hn_dtype=hn_dtype)


def _matmul_kernel(x_ref, w_ref, o_ref):
    o_ref[...] = jnp.dot(x_ref[...], w_ref[...], preferred_element_type=F32).astype(o_ref.dtype)


def _qkv_proj(x, w):
    n, d = x.shape
    n3 = w.shape[1]
    tm, tn = min(QKV_ROW_TILE, n), min(QKV_COL_TILE, n3)
    return pl.pallas_call(
        _matmul_kernel,
        out_shape=jax.ShapeDtypeStruct((n, n3), BF16),
        grid=(n3 // tn, n // tm),
        in_specs=[pl.BlockSpec((tm, d), lambda j, i: (i, 0)),
                  pl.BlockSpec((d, tn), lambda j, i: (0, j))],
        out_specs=pl.BlockSpec((tm, tn), lambda j, i: (i, j)),
        compiler_params=_cparams(("parallel", "parallel")),
        name="qkv_proj",
    )(x, w)


def _attn_out_kernel(a_ref, wo_ref, h_ref, gf_ref, wr_ref, br_ref,
                     ho_ref, hn_ref, eidx_ref, wts_ref, cnt_ref, carry_ref):
    @pl.when(pl.program_id(0) == 0)
    def _():
        carry_ref[...] = jnp.zeros_like(carry_ref)

    h = h_ref[...] + jnp.dot(a_ref[...], wo_ref[...], preferred_element_type=F32)
    ho_ref[...] = h
    _route_tail(h, gf_ref, wr_ref, br_ref, carry_ref, hn_ref, eidx_ref, wts_ref, cnt_ref)


def _attn_out(a, wo, h, g_ffn, wr, br):
    n, d = h.shape
    tm = ROW_TILE
    ep = wr.shape[1]
    return pl.pallas_call(
        _attn_out_kernel,
        out_shape=_route_out_shapes(n, d, ep),
        grid=(n // tm,),
        in_specs=[pl.BlockSpec((tm, d), lambda i: (i, 0)),
                  pl.BlockSpec((d, d), lambda i: (0, 0)),
                  pl.BlockSpec((tm, d), lambda i: (i, 0)),
                  pl.BlockSpec((1, d), lambda i: (0, 0)),
                  pl.BlockSpec((d, ep), lambda i: (0, 0)),
                  pl.BlockSpec((1, ep), lambda i: (0, 0))],
        out_specs=_route_out_specs(tm, d, ep),
        scratch_shapes=[pltpu.VMEM((1, ep), F32)],
        compiler_params=_cparams(("arbitrary",)),
        name="attn_out_proj",
    )(a, wo, h, g_ffn, wr, br)


def _attention_kernel(q_ref, k_ref, v_ref, bias_ref, g_ref, lam_ref, o_ref,
                      m_ref, l_ref, acc_ref, *, exp_mult, out_mult):
    i = pl.program_id(1)
    tq = q_ref.shape[0]
    tk = tq
    dh = q_ref.shape[1] // 2
    nt = (((1,), (1,)), ((), ()))
    dv = v_ref.shape[1]

    m_ref[...] = jnp.full_like(m_ref, M_INIT)
    l_ref[...] = jnp.zeros_like(l_ref)
    acc_ref[...] = jnp.zeros_like(acc_ref)

    def tile(j, bias_cols):
        start = pl.multiple_of(j * tk, tk)
        vt = v_ref[pl.ds(start, tk), :]
        for mp in range(2):
            s = lax.dot_general(q_ref[:, mp * dh:(mp + 1) * dh], k_ref[pl.ds(start, tk), mp * dh:(mp + 1) * dh],
                                nt, preferred_element_type=F32)
            if bias_cols is not None:
                s = s + bias_ref[:, bias_cols:bias_cols + tk]
            m_old = m_ref[mp]
            m_new = jnp.maximum(m_old, jnp.max(s, axis=-1, keepdims=True))
            alpha = jnp.exp2((m_old - m_new) * exp_mult)
            ps = [jnp.exp2((s[:, t * LANES:(t + 1) * LANES] - m_new) * exp_mult) for t in range(tk // LANES)]
            psum = ps[0]
            for pt in ps[1:]:
                psum = psum + pt
            l_ref[mp] = alpha * l_ref[mp] + psum
            pv = jnp.dot(jnp.concatenate(ps, axis=-1).astype(BF16), vt, preferred_element_type=F32)
            for t in range(dv // LANES):
                cols = slice(t * LANES, (t + 1) * LANES)
                acc_ref[mp, :, cols] = alpha * acc_ref[mp, :, cols] + pv[:, cols]
            m_ref[mp] = m_new

    def far(j, carry):
        tile(j, None)
        return carry

    lax.fori_loop(0, i - 1, far, 0)

    @pl.when(i >= 1)
    def _():
        tile(i - 1, 0)

    tile(i, tk)

    lam = lam_ref[0]
    l0 = jnp.sum(l_ref[0], axis=-1, keepdims=True)
    l1 = jnp.sum(l_ref[1], axis=-1, keepdims=True)
    o = acc_ref[0] / l0 - lam * (acc_ref[1] / l1)
    o = o * lax.rsqrt(jnp.mean(o * o, axis=-1, keepdims=True) + SUBLN_EPS) * g_ref[...]
    o_ref[...] = (o * out_mult).astype(o_ref.dtype)


def _attention(qkv, lam, bias, subln_g, *, heads, dh, lambda_init):
    n = qkv.shape[0]
    dv = 2 * dh
    d = heads * dv
    tq = ATT_TILE
    scale = dh ** -0.5
    kern = functools.partial(_attention_kernel, exp_mult=scale * math.log2(math.e),
                             out_mult=1.0 - lambda_init)
    return pl.pallas_call(
        kern,
        out_shape=jax.ShapeDtypeStruct((n, d), BF16),
        grid=(heads, n // tq),
        in_specs=[pl.BlockSpec((tq, dv), lambda h, i: (i, h)),
                  pl.BlockSpec((n, dv), lambda h, i: (0, heads + h)),
                  pl.BlockSpec((n, dv), lambda h, i: (0, 2 * heads + h)),
                  pl.BlockSpec((None, tq, 2 * tq), lambda h, i: (h, 0, 0)),
                  pl.BlockSpec((1, dv), lambda h, i: (0, 0)),
                  pl.BlockSpec(memory_space=pltpu.SMEM)],
        out_specs=pl.BlockSpec((tq, dv), lambda h, i: (i, h)),
        scratch_shapes=[pltpu.VMEM((2, tq, LANES), F32), pltpu.VMEM((2, tq, LANES), F32),
                        pltpu.VMEM((2, tq, dv), F32)],
        compiler_params=_cparams(("arbitrary", "arbitrary")),
        name="diff_attention",
    )(qkv, qkv, qkv, bias, subln_g, lam)


def _rel_bucket(rel):
    nb = REL_BUCKETS // 2
    ret = (rel > 0).astype(I32) * nb
    n = jnp.abs(rel)
    max_exact = nb // 2
    large = max_exact + (jnp.log(jnp.maximum(n, 1).astype(F32) / max_exact)
                         / math.log(REL_MAX_DISTANCE / max_exact) * (nb - max_exact)).astype(I32)
    large = jnp.minimum(large, nb - 1)
    return ret + jnp.where(n < max_exact, n, large)


def _near_bias(rel_bias, dh):
    tq = ATT_TILE
    scale = dh ** -0.5
    table = rel_bias.astype(F32)
    r = jnp.arange(tq, dtype=I32)[:, None]
    c = jnp.arange(2 * tq, dtype=I32)[None, :] - tq
    bias = table[_rel_bucket(c - r)]
    far = table[_rel_bucket(jnp.full((), -(2 * tq), I32))]
    allowed = jnp.floor_divide(c, CHUNK) <= (r // CHUNK)
    pat = jnp.where(allowed[:, :, None], (bias - far) / scale, MASK_VALUE / scale)
    return pat.transpose(2, 0, 1)


def kernel(x, norm_mix_g, norm_ffn_g, final_norm_g, pool_w, pool_scale, diff_w_qkv, diff_w_o,
           diff_lambda_q1, diff_lambda_k1, diff_lambda_q2, diff_lambda_k2, diff_subln_g, rel_bias,
           moe_w_router, moe_b_router, moe_w_gate_up, moe_b_gate_up, moe_w_down, moe_b_down):
    b, s, d = x.shape
    assert b == 1, "token tiles assume a single sequence"
    assert norm_mix_g.shape[0] == 2, "layer 0 pools, layer 1 attends"
    e = moe_w_router.shape[2]
    heads = rel_bias.shape[1]
    dh = diff_lambda_q1.shape[1]
    assert s % ROW_TILE == 0 and s % ATT_TILE == 0 and ATT_TILE % CHUNK == 0 and ATT_TILE >= REL_MAX_DISTANCE
    ep = -(-e // LANES) * LANES
    xt = x.reshape(s, d)

    def router(i):
        wr = jnp.pad(moe_w_router[i], ((0, 0), (0, ep - e))).astype(BF16)
        br = jnp.pad(moe_b_router[i].astype(F32), (0, ep - e), constant_values=MASK_VALUE).reshape(1, ep)
        return wr, br

    row = lambda v: v.astype(F32).reshape(1, -1)

    wr, br = router(0)
    h, hn, eidx, wts, cnt = _pool_mixer(xt, row(norm_mix_g[0]), pool_w[0].astype(BF16), row(pool_scale[0]),
                                        row(norm_ffn_g[0]), wr, br)
    h, hn = _moe(h, hn, eidx, wts, cnt, moe_w_gate_up, moe_b_gate_up, moe_w_down, moe_b_down, 0,
                 row(norm_mix_g[1]), emit_h=True, hn_dtype=BF16)

    lambda_init = 0.8 - 0.6 * math.exp(-0.3 * 1)
    lam = (jnp.exp(jnp.sum(diff_lambda_q1[0].astype(F32) * diff_lambda_k1[0].astype(F32)))
           - jnp.exp(jnp.sum(diff_lambda_q2[0].astype(F32) * diff_lambda_k2[0].astype(F32))) + lambda_init)
    qkv = _qkv_proj(hn, diff_w_qkv[0].astype(BF16))
    att = _attention(qkv, lam.reshape(1), _near_bias(rel_bias, dh), row(diff_subln_g[0]),
                     heads=heads, dh=dh, lambda_init=lambda_init)
    wr, br = router(1)
    h, hn, eidx, wts, cnt = _attn_out(att, diff_w_o[0].astype(BF16), h, row(norm_ffn_g[1]), wr, br)
    (out,) = _moe(h, hn, eidx, wts, cnt, moe_w_gate_up, moe_b_gate_up, moe_w_down, moe_b_down, 1,
                  row(final_norm_g), emit_h=False, hn_dtype=x.dtype)
    return out.reshape(b, s, d)
```

```python
import functools
import math

import jax
import jax.numpy as jnp
from jax import lax
from jax.experimental import pallas as pl
from jax.experimental.pallas import tpu as pltpu

F32 = jnp.float32
BF16 = jnp.bfloat16
I32 = jnp.int32

TOP_K = 4
CHUNK = 64
POOL_WINDOWS = (2, 4, 8, 16)
REL_BUCKETS = 32
REL_MAX_DISTANCE = 128
SWIGLU_LIMIT = 7.0
SWIGLU_ALPHA = 1.702
NORM_EPS = 1e-6
SUBLN_EPS = 1e-5

LANES = 128
SUBLANES = 8
VMEM_LIMIT_BYTES = 60000 * 1024
ROW_TILE = 256
HALO = 16
GATE_UP_TILE = 1024
DOWN_TILE = 2048
ATT_TILE = 512
QKV_ROW_TILE = 1024
QKV_COL_TILE = 768
MASK_VALUE = -1e30
M_INIT = -1e37


def _cparams(sem):
    return pltpu.CompilerParams(dimension_semantics=sem, vmem_limit_bytes=VMEM_LIMIT_BYTES)


def _rms(x, g, eps):
    return x * lax.rsqrt(jnp.mean(x * x, axis=-1, keepdims=True) + eps) * g


def _route_tail(h, gf_ref, wr_ref, br_ref, carry_ref, hn_ref, eidx_ref, wts_ref, cnt_ref):
    tm = h.shape[0]
    ep = wr_ref.shape[1]
    hn = _rms(h, gf_ref[...], NORM_EPS)
    hn_ref[...] = hn
    logits = jnp.dot(hn.astype(BF16), wr_ref[...], preferred_element_type=F32) + br_ref[...]
    lane = lax.broadcasted_iota(I32, (tm, ep), 1)
    work = logits
    sel_idx, sel_val, sel_oh = [], [], []
    for _ in range(TOP_K):
        m = jnp.max(work, axis=-1, keepdims=True)
        idx = jnp.min(jnp.where(work == m, lane, ep), axis=-1, keepdims=True)
        oh = lane == idx
        work = jnp.where(oh, -jnp.inf, work)
        sel_idx.append(idx)
        sel_val.append(m)
        sel_oh.append(oh)
    ex = [jnp.exp(v - sel_val[0]) for v in sel_val]
    den = ex[0] + ex[1] + ex[2] + ex[3]
    oh_all = jnp.zeros((tm, ep), F32)
    for oh in sel_oh:
        oh_all = oh_all + oh.astype(F32)
    r = lax.broadcasted_iota(I32, (tm, tm), 0)
    c = lax.broadcasted_iota(I32, (tm, tm), 1)
    tri = jnp.where(c < r, 1.0, 0.0).astype(BF16)
    before = jnp.dot(tri, oh_all.astype(BF16), preferred_element_type=F32) + carry_ref[...]
    eout = jnp.zeros((tm, LANES), I32)
    wout = jnp.zeros((tm, LANES), F32)
    lane_o = lax.broadcasted_iota(I32, (tm, LANES), 1)
    for k in range(TOP_K):
        rank = jnp.sum(jnp.where(sel_oh[k], before, 0.0), axis=-1, keepdims=True).astype(I32)
        eout = jnp.where(lane_o == k, sel_idx[k], eout)
        eout = jnp.where(lane_o == TOP_K + k, rank, eout)
        wout = jnp.where(lane_o == k, ex[k] / den, wout)
    eidx_ref[...] = eout
    wts_ref[...] = wout
    carry_ref[...] = carry_ref[...] + jnp.sum(oh_all, axis=0, keepdims=True)
    cnt_ref[...] = carry_ref[...]


def _route_out_shapes(n, d, ep):
    return (jax.ShapeDtypeStruct((n, d), F32),
            jax.ShapeDtypeStruct((n, d), F32),
            jax.ShapeDtypeStruct((n, LANES), I32),
            jax.ShapeDtypeStruct((n, LANES), F32),
            jax.ShapeDtypeStruct((1, ep), F32))


def _route_out_specs(tm, d, ep):
    return (pl.BlockSpec((tm, d), lambda i: (i, 0)),
            pl.BlockSpec((tm, d), lambda i: (i, 0)),
            pl.BlockSpec((tm, LANES), lambda i: (i, 0)),
            pl.BlockSpec((tm, LANES), lambda i: (i, 0)),
            pl.BlockSpec((1, ep), lambda i: (0, 0)))


def _pool_mixer_kernel(x_ref, xh_ref, gm_ref, pw_ref, ps_ref, gf_ref, wr_ref, br_ref,
                       h_ref, hn_ref, eidx_ref, wts_ref, cnt_ref, ext_ref, carry_ref):
    i = pl.program_id(0)
    tm, d = x_ref.shape
    g = d // len(POOL_WINDOWS)

    @pl.when(i == 0)
    def _():
        carry_ref[...] = jnp.zeros_like(carry_ref)

    x = x_ref[...]
    gm = gm_ref[...]
    hn = _rms(x, gm, NORM_EPS)
    halo = _rms(xh_ref[...], gm, NORM_EPS)
    ext_ref[0:HALO, :] = jnp.where(i == 0, 0.0, halo)
    ext_ref[HALO:, :] = hn
    pos = i * tm + lax.broadcasted_iota(I32, (tm, 1), 0)
    outs = []
    for gi, w in enumerate(POOL_WINDOWS):
        c0 = gi * g
        tok = hn[:, c0:c0 + g]
        acc = tok
        for j in range(1, w):
            acc = acc + ext_ref[HALO - j:HALO - j + tm, c0:c0 + g]
        cnt = jnp.minimum(pos + 1, w).astype(F32)
        p = acc / cnt - tok
        outs.append(jnp.dot(p.astype(BF16), pw_ref[gi], preferred_element_type=F32))
    h = x + jnp.concatenate(outs, axis=-1) * ps_ref[...]
    h_ref[...] = h
    _route_tail(h, gf_ref, wr_ref, br_ref, carry_ref, hn_ref, eidx_ref, wts_ref, cnt_ref)


def _pool_mixer(x, g_mix, pool_w, pool_scale, g_ffn, wr, br):
    n, d = x.shape
    tm = ROW_TILE
    ep = wr.shape[1]
    ng, g, _ = pool_w.shape
    hb = tm // HALO
    return pl.pallas_call(
        _pool_mixer_kernel,
        out_shape=_route_out_shapes(n, d, ep),
        grid=(n // tm,),
        in_specs=[pl.BlockSpec((tm, d), lambda i: (i, 0)),
                  pl.BlockSpec((HALO, d), lambda i: (jnp.maximum(i * hb - 1, 0), 0)),
                  pl.BlockSpec((1, d), lambda i: (0, 0)),
                  pl.BlockSpec((ng, g, g), lambda i: (0, 0, 0)),
                  pl.BlockSpec((1, d), lambda i: (0, 0)),
                  pl.BlockSpec((1, d), lambda i: (0, 0)),
                  pl.BlockSpec((d, ep), lambda i: (0, 0)),
                  pl.BlockSpec((1, ep), lambda i: (0, 0))],
        out_specs=_route_out_specs(tm, d, ep),
        scratch_shapes=[pltpu.VMEM((HALO + tm, d), F32), pltpu.VMEM((1, ep), F32)],
        compiler_params=_cparams(("arbitrary",)),
        name="pool_mixer",
    )(x, x, g_mix, pool_w, pool_scale, g_ffn, wr, br)


def _dispatch_kernel(fill_ref, nu_ref, dest_ref, x_ref, xg_ref, zero_ref, sem, zsem):
    tm = x_ref.shape[0]
    n_blocks = xg_ref.shape[0] // tm

    @pl.when(pl.program_id(0) == 0)
    def _():
        zero_ref[...] = jnp.zeros_like(zero_ref)

        def zero_rows(first):
            cp = pltpu.make_async_copy(zero_ref, xg_ref.at[pl.ds(first, tm)], zsem)
            cp.start()
            cp.wait()

        def pad(e, carry):
            zero_rows(pl.multiple_of(fill_ref[e], tm))
            return carry

        def tail(b, carry):
            zero_rows(pl.multiple_of(b * tm, tm))
            return carry

        lax.fori_loop(0, fill_ref.shape[0], pad, 0)
        lax.fori_loop(nu_ref[0], n_blocks, tail, 0)

    def row_copy(t, k):
        d = dest_ref[0, 0, TOP_K * t + k]
        return pltpu.make_async_copy(x_ref.at[pl.ds(t, 1)], xg_ref.at[pl.ds(d, 1)], sem)

    def start(t, carry):
        for k in range(TOP_K):
            row_copy(t, k).start()
        return carry

    def wait(t, carry):
        for k in range(TOP_K):
            row_copy(t, k).wait()
        return carry

    lax.fori_loop(0, tm, start, 0)
    lax.fori_loop(0, tm, wait, 0)


def _dispatch(fill_start, n_used, dest, hn, n_rows):
    n, d = hn.shape
    tm = ROW_TILE
    grid_spec = pltpu.PrefetchScalarGridSpec(
        num_scalar_prefetch=2,
        grid=(n // tm,),
        in_specs=[pl.BlockSpec((1, 1, TOP_K * tm), lambda i, fs, nu: (i, 0, 0), memory_space=pltpu.SMEM),
                  pl.BlockSpec((tm, d), lambda i, fs, nu: (i, 0))],
        out_specs=pl.BlockSpec(memory_space=pl.ANY),
        scratch_shapes=[pltpu.VMEM((tm, d), hn.dtype), pltpu.SemaphoreType.DMA, pltpu.SemaphoreType.DMA])
    return pl.pallas_call(
        _dispatch_kernel,
        out_shape=jax.ShapeDtypeStruct((n_rows, d), hn.dtype),
        grid_spec=grid_spec,
        compiler_params=_cparams(("arbitrary",)),
        name="moe_dispatch",
    )(fill_start, n_used, dest.reshape(n // tm, 1, TOP_K * tm), hn)


def _expert_changed(be_ref, i):
    return jnp.logical_or(i == 0, be_ref[i] != be_ref[jnp.maximum(i - 1, 0)])


def _gate_up_kernel(be_ref, nu_ref, x_ref, wg_ref, wl_ref, bg_ref, bl_ref, a_ref, wg_bf, wl_bf):
    i = pl.program_id(1)

    @pl.when(i < nu_ref[0])
    def _():
        @pl.when(_expert_changed(be_ref, i))
        def _():
            wg_bf[...] = wg_ref[...].astype(BF16)
            wl_bf[...] = wl_ref[...].astype(BF16)

        x = x_ref[...].astype(BF16)
        gate = jnp.dot(x, wg_bf[...], preferred_element_type=F32) + bg_ref[...]
        lin = jnp.dot(x, wl_bf[...], preferred_element_type=F32) + bl_ref[...]
        glu = jnp.minimum(gate, SWIGLU_LIMIT)
        lin = jnp.clip(lin, -SWIGLU_LIMIT, SWIGLU_LIMIT)
        a_ref[...] = (glu * jax.nn.sigmoid(SWIGLU_ALPHA * glu) * (lin + 1.0)).astype(a_ref.dtype)

    @pl.when(i >= nu_ref[0])
    def _():
        a_ref[...] = jnp.zeros_like(a_ref)


def _gate_up(block_e, n_used, xg, w_gate_up, b_gate_up, layer):
    n_rows, d = xg.shape
    nl, e, _, f2 = w_gate_up.shape
    f = f2 // 2
    rb, tn = ROW_TILE, min(GATE_UP_TILE, f)
    nj = f // tn

    def row(i, nu):
        return jnp.minimum(i, nu[0] - 1)

    grid_spec = pltpu.PrefetchScalarGridSpec(
        num_scalar_prefetch=2,
        grid=(nj, n_rows // rb),
        in_specs=[pl.BlockSpec((rb, d), lambda j, i, be, nu: (row(i, nu), 0)),
                  pl.BlockSpec((None, None, d, tn), lambda j, i, be, nu: (layer, be[row(i, nu)], 0, j)),
                  pl.BlockSpec((None, None, d, tn), lambda j, i, be, nu: (layer, be[row(i, nu)], 0, j + nj)),
                  pl.BlockSpec((None, None, 1, tn), lambda j, i, be, nu: (layer, be[row(i, nu)], 0, j)),
                  pl.BlockSpec((None, None, 1, tn), lambda j, i, be, nu: (layer, be[row(i, nu)], 0, j + nj))],
        out_specs=pl.BlockSpec((rb, tn), lambda j, i, be, nu: (i, j)),
        scratch_shapes=[pltpu.VMEM((d, tn), BF16), pltpu.VMEM((d, tn), BF16)])
    b3 = b_gate_up.reshape(nl, e, 1, f2)
    return pl.pallas_call(
        _gate_up_kernel,
        out_shape=jax.ShapeDtypeStruct((n_rows, f), BF16),
        grid_spec=grid_spec,
        compiler_params=_cparams(("arbitrary", "arbitrary")),
        name="moe_gate_up",
    )(block_e, n_used, xg, w_gate_up, w_gate_up, b3, b3)


def _down_kernel(be_ref, nu_ref, a_ref, wd_ref, bd_ref, y_ref, wd_bf):
    i = pl.program_id(1)

    @pl.when(i < nu_ref[0])
    def _():
        @pl.when(_expert_changed(be_ref, i))
        def _():
            wd_bf[...] = wd_ref[...].astype(BF16)

        y_ref[...] = jnp.dot(a_ref[...], wd_bf[...], preferred_element_type=F32) + bd_ref[...]

    @pl.when(i >= nu_ref[0])
    def _():
        y_ref[...] = jnp.zeros_like(y_ref)


def _down(block_e, n_used, a, w_down, b_down, layer):
    n_rows, f = a.shape
    nl, e, _, d = w_down.shape
    rb, tn = ROW_TILE, min(DOWN_TILE, d)

    def row(i, nu):
        return jnp.minimum(i, nu[0] - 1)

    grid_spec = pltpu.PrefetchScalarGridSpec(
        num_scalar_prefetch=2,
        grid=(d // tn, n_rows // rb),
        in_specs=[pl.BlockSpec((rb, f), lambda j, i, be, nu: (row(i, nu), 0)),
                  pl.BlockSpec((None, None, f, tn), lambda j, i, be, nu: (layer, be[row(i, nu)], 0, j)),
                  pl.BlockSpec((None, None, 1, tn), lambda j, i, be, nu: (layer, be[row(i, nu)], 0, j))],
        out_specs=pl.BlockSpec((rb, tn), lambda j, i, be, nu: (i, j)),
        scratch_shapes=[pltpu.VMEM((f, tn), BF16)])
    return pl.pallas_call(
        _down_kernel,
        out_shape=jax.ShapeDtypeStruct((n_rows, d), F32),
        grid_spec=grid_spec,
        compiler_params=_cparams(("arbitrary", "arbitrary")),
        name="moe_down",
    )(block_e, n_used, a, w_down, b_down.reshape(nl, e, 1, d))


def _combine_kernel(dcur_ref, dnext_ref, wts_ref, h_ref, g_ref, y_ref, *rest, emit_h):
    if emit_h:
        h_out_ref, hn_out_ref, buf, sem = rest
    else:
        hn_out_ref, buf, sem = rest
    i = pl.program_id(0)
    n = pl.num_programs(0)
    tm = h_ref.shape[0]

    def row_copy(dest_ref, slot, t, k):
        d = dest_ref[0, 0, TOP_K * t + k]
        return pltpu.make_async_copy(y_ref.at[pl.ds(d, 1)], buf.at[slot, k, pl.ds(t, 1)], sem.at[slot])

    def start_tile(dest_ref, slot):
        def body(t, carry):
            for k in range(TOP_K):
                row_copy(dest_ref, slot, t, k).start()
            return carry
        lax.fori_loop(0, tm, body, 0)

    @pl.when(i == 0)
    def _():
        start_tile(dcur_ref, 0)

    @pl.when(i + 1 < n)
    def _():
        start_tile(dnext_ref, (i + 1) % 2)

    slot = i % 2

    def wait_body(t, carry):
        for k in range(TOP_K):
            row_copy(dcur_ref, slot, t, k).wait()
        return carry
    lax.fori_loop(0, tm, wait_body, 0)

    w = wts_ref[...]
    acc = h_ref[...]
    for k in range(TOP_K):
        acc = acc + w[:, k:k + 1] * buf[slot, k]
    if emit_h:
        h_out_ref[...] = acc
    hn_out_ref[...] = _rms(acc, g_ref[...], NORM_EPS).astype(hn_out_ref.dtype)


def _combine(dest, wts, h, g_next, y, *, emit_h, hn_dtype):
    n, d = h.shape
    tm = ROW_TILE
    nt = n // tm
    dest3 = dest.reshape(nt, 1, TOP_K * tm)
    tile = pl.BlockSpec((tm, d), lambda i: (i, 0))
    out_shape = [jax.ShapeDtypeStruct((n, d), hn_dtype)]
    out_specs = [tile]
    if emit_h:
        out_shape = [jax.ShapeDtypeStruct((n, d), F32)] + out_shape
        out_specs = [tile] + out_specs
    return pl.pallas_call(
        functools.partial(_combine_kernel, emit_h=emit_h),
        out_shape=tuple(out_shape),
        grid=(nt,),
        in_specs=[pl.BlockSpec((1, 1, TOP_K * tm), lambda i: (i, 0, 0), memory_space=pltpu.SMEM),
                  pl.BlockSpec((1, 1, TOP_K * tm), lambda i: (jnp.minimum(i + 1, nt - 1), 0, 0),
                               memory_space=pltpu.SMEM),
                  pl.BlockSpec((tm, LANES), lambda i: (i, 0)),
                  tile,
                  pl.BlockSpec((1, d), lambda i: (0, 0)),
                  pl.BlockSpec(memory_space=pl.ANY)],
        out_specs=tuple(out_specs),
        scratch_shapes=[pltpu.VMEM((2, TOP_K, tm, d), F32), pltpu.SemaphoreType.DMA((2,))],
        compiler_params=_cparams(("arbitrary",)),
        name="moe_combine",
    )(dest3, dest3, wts, h, g_next, y)


def _moe(h, hn, eidx, wts, cnt, w_gate_up, b_gate_up, w_down, b_down, layer, g_next, *, emit_h, hn_dtype):
    n, _ = h.shape
    e = w_gate_up.shape[1]
    rb = ROW_TILE
    n_blocks = -(-(n * TOP_K + e * (rb - 1)) // rb)
    counts = cnt[0, :e].astype(I32)
    padded = (counts + rb - 1) // rb * rb
    ids = jnp.arange(e, dtype=I32)
    pend = jnp.sum(jnp.where(ids[None, :] <= ids[:, None], padded[None, :], 0), axis=1)
    pstart = pend - padded
    sel = eidx[:, :TOP_K, None] == ids[None, None, :]
    dest = jnp.sum(jnp.where(sel, pstart[None, None, :], 0), axis=-1) + eidx[:, TOP_K:2 * TOP_K]
    first_row = jnp.arange(n_blocks, dtype=I32) * rb
    block_e = jnp.minimum(jnp.sum((pend[None, :] <= first_row[:, None]).astype(I32), axis=1), e - 1)
    n_used = pend[-1:] // rb
    xg = _dispatch(jnp.maximum(pend - rb, 0), n_used, dest, hn, n_blocks * rb)
    a = _gate_up(block_e, n_used, xg, w_gate_up, b_gate_up, layer)
    y = _down(block_e, n_used, a, w_down, b_down, layer)
    return _combine(dest, wts, h, g_next, y, emit_h=emit_h, hn_dtype=hn_dtype)


def _matmul_kernel(x_ref, w_ref, o_ref):
    o_ref[...] = jnp.dot(x_ref[...], w_ref[...], preferred_element_type=F32).astype(o_ref.dtype)


def _qkv_proj(x, w):
    n, d = x.shape
    n3 = w.shape[1]
    tm, tn = min(QKV_ROW_TILE, n), min(QKV_COL_TILE, n3)
    return pl.pallas_call(
        _matmul_kernel,
        out_shape=jax.ShapeDtypeStruct((n, n3), BF16),
        grid=(n3 // tn, n // tm),
        in_specs=[pl.BlockSpec((tm, d), lambda j, i: (i, 0)),
                  pl.BlockSpec((d, tn), lambda j, i: (0, j))],
        out_specs=pl.BlockSpec((tm, tn), lambda j, i: (i, j)),
        compiler_params=_cparams(("parallel", "parallel")),
        name="qkv_proj",
    )(x, w)


def _attn_out_kernel(a_ref, wo_ref, h_ref, gf_ref, wr_ref, br_ref,
                     ho_ref, hn_ref, eidx_ref, wts_ref, cnt_ref, carry_ref):
    @pl.when(pl.program_id(0) == 0)
    def _():
        carry_ref[...] = jnp.zeros_like(carry_ref)

    h = h_ref[...] + jnp.dot(a_ref[...], wo_ref[...], preferred_element_type=F32)
    ho_ref[...] = h
    _route_tail(h, gf_ref, wr_ref, br_ref, carry_ref, hn_ref, eidx_ref, wts_ref, cnt_ref)


def _attn_out(a, wo, h, g_ffn, wr, br):
    n, d = h.shape
    tm = ROW_TILE
    ep = wr.shape[1]
    return pl.pallas_call(
        _attn_out_kernel,
        out_shape=_route_out_shapes(n, d, ep),
        grid=(n // tm,),
        in_specs=[pl.BlockSpec((tm, d), lambda i: (i, 0)),
                  pl.BlockSpec((d, d), lambda i: (0, 0)),
                  pl.BlockSpec((tm, d), lambda i: (i, 0)),
                  pl.BlockSpec((1, d), lambda i: (0, 0)),
                  pl.BlockSpec((d, ep), lambda i: (0, 0)),
                  pl.BlockSpec((1, ep), lambda i: (0, 0))],
        out_specs=_route_out_specs(tm, d, ep),
        scratch_shapes=[pltpu.VMEM((1, ep), F32)],
        compiler_params=_cparams(("arbitrary",)),
        name="attn_out_proj",
    )(a, wo, h, g_ffn, wr, br)


def _attention_kernel(q_ref, k_ref, v_ref, bias_ref, g_ref, lam_ref, o_ref,
                      m_ref, l_ref, acc_ref, s_ref, *, exp_mult, out_mult):
    i = pl.program_id(1)
    tq = q_ref.shape[0]
    tk = tq
    dh = q_ref.shape[1] // 2
    nt = (((1,), (1,)), ((), ()))
    dv = v_ref.shape[1]

    m_ref[...] = jnp.full_like(m_ref, M_INIT)
    l_ref[...] = jnp.zeros_like(l_ref)
    acc_ref[...] = jnp.zeros_like(acc_ref)

    def scores(j, slot):
        start = pl.multiple_of(j * tk, tk)
        for mp in range(2):
            s_ref[slot, mp] = lax.dot_general(q_ref[:, mp * dh:(mp + 1) * dh],
                                              k_ref[pl.ds(start, tk), mp * dh:(mp + 1) * dh],
                                              nt, preferred_element_type=F32)

    def softmax_pv(j, slot, bias_cols):
        start = pl.multiple_of(j * tk, tk)
        vt = v_ref[pl.ds(start, tk), :]
        for mp in range(2):
            s = s_ref[slot, mp]
            if bias_cols is not None:
                s = s + bias_ref[:, bias_cols:bias_cols + tk]
            m_old = m_ref[mp]
            m_new = jnp.maximum(m_old, jnp.max(s, axis=-1, keepdims=True))
            alpha = jnp.exp2((m_old - m_new) * exp_mult)
            ps = [jnp.exp2((s[:, t * LANES:(t + 1) * LANES] - m_new) * exp_mult) for t in range(tk // LANES)]
            psum = ps[0]
            for pt in ps[1:]:
                psum = psum + pt
            l_ref[mp] = alpha * l_ref[mp] + psum
            pv = jnp.dot(jnp.concatenate(ps, axis=-1).astype(BF16), vt, preferred_element_type=F32)
            for t in range(dv // LANES):
                cols = slice(t * LANES, (t + 1) * LANES)
                acc_ref[mp, :, cols] = alpha * acc_ref[mp, :, cols] + pv[:, cols]
            m_ref[mp] = m_new

    def step(j, slot, bias_cols):
        scores(j + 1, 1 - slot)
        softmax_pv(j, slot, bias_cols)

    n_far = jnp.maximum(i - 1, 0)
    odd = n_far % 2
    scores(0, i % 2)

    @pl.when(odd == 1)
    def _():
        step(0, 0, None)

    def far_pair(t, carry):
        j = odd + 2 * t
        step(j, 1, None)
        step(j + 1, 0, None)
        return carry

    lax.fori_loop(0, n_far // 2, far_pair, 0)

    @pl.when(i >= 1)
    def _():
        step(i - 1, 1, 0)

    softmax_pv(i, 0, tk)

    lam = lam_ref[0]
    l0 = jnp.sum(l_ref[0], axis=-1, keepdims=True)
    l1 = jnp.sum(l_ref[1], axis=-1, keepdims=True)
    o = acc_ref[0] / l0 - lam * (acc_ref[1] / l1)
    o = o * lax.rsqrt(jnp.mean(o * o, axis=-1, keepdims=True) + SUBLN_EPS) * g_ref[...]
    o_ref[...] = (o * out_mult).astype(o_ref.dtype)


def _attention(qkv, lam, bias, subln_g, *, heads, dh, lambda_init):
    n = qkv.shape[0]
    dv = 2 * dh
    d = heads * dv
    tq = ATT_TILE
    scale = dh ** -0.5
    kern = functools.partial(_attention_kernel, exp_mult=scale * math.log2(math.e),
                             out_mult=1.0 - lambda_init)
    return pl.pallas_call(
        kern,
        out_shape=jax.ShapeDtypeStruct((n, d), BF16),
        grid=(heads, n // tq),
        in_specs=[pl.BlockSpec((tq, dv), lambda h, i: (i, h)),
                  pl.BlockSpec((n, dv), lambda h, i: (0, heads + h)),
                  pl.BlockSpec((n, dv), lambda h, i: (0, 2 * heads + h)),
                  pl.BlockSpec((None, tq, 2 * tq), lambda h, i: (h, 0, 0)),
                  pl.BlockSpec((1, dv), lambda h, i: (0, 0)),
                  pl.BlockSpec(memory_space=pltpu.SMEM)],
        out_specs=pl.BlockSpec((tq, dv), lambda h, i: (i, h)),
        scratch_shapes=[pltpu.VMEM((2, tq, LANES), F32), pltpu.VMEM((2, tq, LANES), F32),
                        pltpu.VMEM((2, tq, dv), F32), pltpu.VMEM((2, 2, tq, tq), F32)],
        compiler_params=_cparams(("arbitrary", "arbitrary")),
        name="diff_attention",
    )(qkv, qkv, qkv, bias, subln_g, lam)


def _rel_bucket(rel):
    nb = REL_BUCKETS // 2
    ret = (rel > 0).astype(I32) * nb
    n = jnp.abs(rel)
    max_exact = nb // 2
    large = max_exact + (jnp.log(jnp.maximum(n, 1).astype(F32) / max_exact)
                         / math.log(REL_MAX_DISTANCE / max_exact) * (nb - max_exact)).astype(I32)
    large = jnp.minimum(large, nb - 1)
    return ret + jnp.where(n < max_exact, n, large)


def _near_bias(rel_bias, dh):
    tq = ATT_TILE
    scale = dh ** -0.5
    table = rel_bias.astype(F32)
    r = jnp.arange(tq, dtype=I32)[:, None]
    c = jnp.arange(2 * tq, dtype=I32)[None, :] - tq
    bucket = _rel_bucket(c - r)[None]
    far = table[_rel_bucket(jnp.full((), -(2 * tq), I32))]
    bias = jnp.zeros((table.shape[1], tq, 2 * tq), F32)
    for b in range(REL_BUCKETS):
        bias = jnp.where(bucket == b, table[b][:, None, None], bias)
    allowed = (jnp.floor_divide(c, CHUNK) <= (r // CHUNK))[None]
    return jnp.where(allowed, (bias - far[:, None, None]) / scale, MASK_VALUE / scale)


def kernel(x, norm_mix_g, norm_ffn_g, final_norm_g, pool_w, pool_scale, diff_w_qkv, diff_w_o,
           diff_lambda_q1, diff_lambda_k1, diff_lambda_q2, diff_lambda_k2, diff_subln_g, rel_bias,
           moe_w_router, moe_b_router, moe_w_gate_up, moe_b_gate_up, moe_w_down, moe_b_down):
    b, s, d = x.shape
    assert b == 1, "token tiles assume a single sequence"
    assert norm_mix_g.shape[0] == 2, "layer 0 pools, layer 1 attends"
    e = moe_w_router.shape[2]
    heads = rel_bias.shape[1]
    dh = diff_lambda_q1.shape[1]
    assert s % ROW_TILE == 0 and s % ATT_TILE == 0 and ATT_TILE % CHUNK == 0 and ATT_TILE >= REL_MAX_DISTANCE
    ep = -(-e // LANES) * LANES
    xt = x.reshape(s, d)

    def router(i):
        wr = jnp.pad(moe_w_router[i], ((0, 0), (0, ep - e))).astype(BF16)
        br = jnp.pad(moe_b_router[i].astype(F32), (0, ep - e), constant_values=MASK_VALUE).reshape(1, ep)
        return wr, br

    row = lambda v: v.astype(F32).reshape(1, -1)

    wr, br = router(0)
    h, hn, eidx, wts, cnt = _pool_mixer(xt, row(norm_mix_g[0]), pool_w[0].astype(BF16), row(pool_scale[0]),
                                        row(norm_ffn_g[0]), wr, br)
    h, hn = _moe(h, hn, eidx, wts, cnt, moe_w_gate_up, moe_b_gate_up, moe_w_down, moe_b_down, 0,
                 row(norm_mix_g[1]), emit_h=True, hn_dtype=BF16)

    lambda_init = 0.8 - 0.6 * math.exp(-0.3 * 1)
    lam = (jnp.exp(jnp.sum(diff_lambda_q1[0].astype(F32) * diff_lambda_k1[0].astype(F32)))
           - jnp.exp(jnp.sum(diff_lambda_q2[0].astype(F32) * diff_lambda_k2[0].astype(F32))) + lambda_init)
    qkv = _qkv_proj(hn, diff_w_qkv[0].astype(BF16))
    att = _attention(qkv, lam.reshape(1), _near_bias(rel_bias, dh), row(diff_subln_g[0]),
                     heads=heads, dh=dh, lambda_init=lambda_init)
    wr, br = router(1)
    h, hn, eidx, wts, cnt = _attn_out(att, diff_w_o[0].astype(BF16), h, row(norm_ffn_g[1]), wr, br)
    (out,) = _moe(h, hn, eidx, wts, cnt, moe_w_gate_up, moe_b_gate_up, moe_w_down, moe_b_down, 1,
                  row(final_norm_g), emit_h=False, hn_dtype=x.dtype)
    return out.reshape(b, s, d)
```
